```python
import jax
import jax.numpy as jnp
from jax import lax
import numpy as np

D_MODEL = 1024
BATCH = 8
SEQ = 4096
DEPTH = 4

GRID_W = 64
CTX_LEN = 256
HEAD_DIM = 64
BRANCH_W = 256
N_BRANCH = 4
A_HQ = 4
A_HKV = 2
C_HQ = 4
C_HKV = 2
WINDOW = 128
Q_BLOCK = 128
F_GROUPS = 4
F_GROUP_W = 64
G_HEADS = 4
G_DK = 64
G_DV = 64
G_CHUNK = 32
N_EXPERTS = 16
N_GROUPS = 4
EXPERTS_PER_GROUP = N_EXPERTS // N_GROUPS
TOP_K = 2
D_EXPERT = 256
ROPE_THETA = 10000.0
EPS = 1e-6
MOD_CHUNKS = 6

IN_WIDTHS = (
    A_HQ * HEAD_DIM, A_HKV * HEAD_DIM, A_HKV * HEAD_DIM,
    F_GROUPS * F_GROUP_W,
    C_HQ * HEAD_DIM, C_HKV * HEAD_DIM, C_HKV * HEAD_DIM,
    G_HEADS * G_DK, G_HEADS * G_DK, G_HEADS * G_DK,
    G_HEADS * G_DV, G_HEADS * G_DV,
    N_BRANCH * D_MODEL,
)
IN_SPLITS = tuple(int(v) for v in np.cumsum(IN_WIDTHS)[:-1])
IN_COLS = int(sum(IN_WIDTHS))

kernel_name = 'hybrid_gated_branch_dit_trunk'


def rms_norm(x, g):
    xf = x.astype(jnp.float32)
    y = xf * lax.rsqrt(jnp.mean(xf * xf, axis=-1, keepdims=True) + EPS)
    return (y * g.astype(jnp.float32)).astype(x.dtype)


def modulate(h, shift, scale):
    return h * (1 + scale) + shift


def split_heads(t, n):
    return t.reshape(t.shape[0], t.shape[1], n, -1)


def axial_rope_tables(rows):
    t = jnp.arange(rows * GRID_W)
    row = (t // GRID_W).astype(jnp.float32)
    col = (t % GRID_W).astype(jnp.float32)
    n_freq = HEAD_DIM // 4
    inv_freq = ROPE_THETA ** (-jnp.arange(n_freq, dtype=jnp.float32) / n_freq)
    ang = jnp.concatenate([row[:, None] * inv_freq, col[:, None] * inv_freq], axis=-1)
    return jnp.cos(ang), jnp.sin(ang)


def apply_rope(x, cos, sin):
    x1, x2 = jnp.split(x.astype(jnp.float32), 2, axis=-1)
    c = cos[None, :, None, :]
    s = sin[None, :, None, :]
    return jnp.concatenate([x1 * c - x2 * s, x1 * s + x2 * c], axis=-1).astype(x.dtype)


def global_attention(q, k, v, k_ctx, v_ctx):
    b_, s_, hq, hd = q.shape
    hkv = k.shape[2]
    grp = hq // hkv
    kk = jnp.concatenate([k, k_ctx], axis=1)
    vv = jnp.concatenate([v, v_ctx], axis=1)
    nb = s_ // Q_BLOCK
    qb = q.reshape(b_, nb, Q_BLOCK, hkv, grp, hd).transpose(1, 0, 2, 3, 4, 5)
    scale = hd ** -0.5

    def one_block(qblk):
        s = jnp.einsum('bqhgd,bkhd->bhgqk', qblk, kk).astype(jnp.float32) * scale
        p = jax.nn.softmax(s, axis=-1)
        return jnp.einsum('bhgqk,bkhd->bqhgd', p.astype(vv.dtype), vv)

    o = lax.map(one_block, qb)
    return o.transpose(1, 0, 2, 3, 4, 5).reshape(b_, s_, hq * hd)


def context_attention(q, k, v, sink):
    b_, t_, hq, hd = q.shape
    hkv = k.shape[2]
    grp = hq // hkv
    qg = q.reshape(b_, t_, hkv, grp, hd)
    s = jnp.einsum('bqhgd,bkhd->bhgqk', qg, k).astype(jnp.float32) * (hd ** -0.5)
    if sink is None:
        p = jax.nn.softmax(s, axis=-1)
    else:
        sk = jnp.broadcast_to(sink.astype(jnp.float32).reshape(hkv, grp)[None, :, :, None, None], s.shape[:-1] + (1,))
        p = jax.nn.softmax(jnp.concatenate([s, sk], axis=-1), axis=-1)[..., :-1]
    o = jnp.einsum('bhgqk,bkhd->bqhgd', p.astype(v.dtype), v)
    return o.reshape(b_, t_, hq * hd)


def window_attention(q, k, v, k_ctx, v_ctx, sink):
    b_, s_, hq, hd = q.shape
    hkv = k.shape[2]
    grp = hq // hkv
    nb = s_ // Q_BLOCK
    t_ = k_ctx.shape[1]
    qb = q.reshape(b_, nb, Q_BLOCK, hkv, grp, hd)

    def neighbours(t):
        tb = t.reshape(b_, nb, Q_BLOCK, hkv, hd)
        tp = jnp.pad(tb, ((0, 0), (1, 1), (0, 0), (0, 0), (0, 0)))
        return jnp.concatenate([tp[:, :-2], tp[:, 1:-1], tp[:, 2:]], axis=2)

    kb = neighbours(k)
    vb = neighbours(v)
    scale = hd ** -0.5
    s_loc = jnp.einsum('bnqhgd,bnkhd->bnhgqk', qb, kb).astype(jnp.float32) * scale
    q_off = jnp.arange(Q_BLOCK)[:, None]
    k_off = jnp.arange(3 * Q_BLOCK)[None, :] - Q_BLOCK
    key_abs = jnp.arange(nb)[:, None, None] * Q_BLOCK + k_off[None]
    valid = (jnp.abs(k_off - q_off) <= WINDOW)[None] & (key_abs >= 0) & (key_abs < s_)
    s_loc = jnp.where(valid[None, :, None, None], s_loc, -jnp.inf)
    s_ctx = jnp.einsum('bnqhgd,bkhd->bnhgqk', qb, k_ctx).astype(jnp.float32) * scale
    s_sink = jnp.broadcast_to(sink.astype(jnp.float32).reshape(hkv, grp)[None, None, :, :, None, None],
                              s_loc.shape[:-1] + (1,))
    p = jax.nn.softmax(jnp.concatenate([s_loc, s_ctx, s_sink], axis=-1), axis=-1)
    p_loc = p[..., :3 * Q_BLOCK].astype(v.dtype)
    p_ctx = p[..., 3 * Q_BLOCK:3 * Q_BLOCK + t_].astype(v.dtype)
    o = jnp.einsum('bnhgqk,bnkhd->bnqhgd', p_loc, vb) + jnp.einsum('bnhgqk,bkhd->bnqhgd', p_ctx, v_ctx)
    return o.reshape(b_, s_, hq * hd)


def fourier_mix(u):
    b_, l_, _ = u.shape
    g = u.astype(jnp.float32).reshape(b_, l_, F_GROUPS, F_GROUP_W)
    y = jnp.real(jnp.fft.fft2(g, axes=(1, 3), norm='ortho'))
    return y.reshape(b_, l_, -1).astype(u.dtype)


def gla_chunked(q, k, v, logf, s0):
    b_, l_, h_, _ = q.shape
    n = l_ // G_CHUNK

    def chunks(t):
        return t.reshape(b_, n, G_CHUNK, h_, t.shape[-1]).transpose(1, 0, 3, 2, 4)

    qc, kc, vc, lf = chunks(q), chunks(k), chunks(v), chunks(logf)
    cum = jnp.cumsum(lf, axis=3)
    cum_last = cum[:, :, :, -1:, :]
    q_in = qc * jnp.exp(cum)
    k_in = kc * jnp.exp(-cum)
    k_out = kc * jnp.exp(cum_last - cum)
    lower = jnp.tril(jnp.ones((G_CHUNK, G_CHUNK), dtype=bool))
    att = jnp.where(lower, jnp.einsum('nbhcd,nbhsd->nbhcs', q_in, k_in), 0.0)
    o_intra = jnp.einsum('nbhcs,nbhse->nbhce', att, vc)
    kv = jnp.einsum('nbhsd,nbhse->nbhde', k_out, vc)
    decay = jnp.exp(cum_last[:, :, :, 0, :])

    def step(state, inp):
        dec, kv_n = inp
        return dec[..., None] * state + kv_n, state

    s_final, s_prev = lax.scan(step, s0, (decay, kv))
    o = o_intra + jnp.einsum('nbhcd,nbhde->nbhce', q_in, s_prev)
    return o.transpose(1, 0, 3, 2, 4).reshape(b_, l_, h_, -1), s_final


def hgrn_scan(q, f_raw, v, lb, s0, reverse):
    f = lb + (1.0 - lb) * jax.nn.sigmoid(f_raw)
    logf = jnp.log(f)
    k = 1.0 - f
    if reverse:
        q, k, v, logf = jnp.flip(q, 1), jnp.flip(k, 1), jnp.flip(v, 1), jnp.flip(logf, 1)
    o, s = gla_chunked(q, k, v, logf, s0)
    if reverse:
        o = jnp.flip(o, 1)
    return o, s


def hgrn_readout(o, g, gain):
    b_, l_ = o.shape[0], o.shape[1]
    return rms_norm(o, gain).reshape(b_, l_, -1).astype(g.dtype) * jax.nn.silu(g)


def hgrn_lower_bounds(logits):
    p = jax.nn.softmax(logits.astype(jnp.float32), axis=1)
    return jnp.cumsum(p, axis=1) - p[:, :1]


def merge_branches(branches, gate, w_branch, w_out):
    gates = jnp.split(jax.nn.sigmoid(gate), N_BRANCH, axis=-1)
    merged = gates[0] * (branches[0] @ w_branch[0])
    for b in range(1, N_BRANCH):
        merged = merged + gates[b] * (branches[b] @ w_branch[b])
    return merged @ w_out


def gh(t):
    return split_heads(t, G_HEADS).astype(jnp.float32)


def mixing_sublayer(h, hc, w_in, qn_a, kn_a, qn_c, kn_c, sink, lb, onorm_g, w_branch, w_out, cos, sin, ctx_out):
    aq, ak, av, fu, cq, ck, cv, gq, gfw, gbw, gi, gg, gate = jnp.split(h @ w_in, IN_SPLITS, axis=-1)
    aq_x, ak_x, av_x, fu_x, cq_x, ck_x, cv_x, gq_x, gfw_x, gbw_x, gi_x, gg_x, gate_x = jnp.split(hc @ w_in, IN_SPLITS, axis=-1)

    ka_x = rms_norm(split_heads(ak_x, A_HKV), kn_a)
    va_x = split_heads(av_x, A_HKV)
    kc_x = rms_norm(split_heads(ck_x, C_HKV), kn_c)
    vc_x = split_heads(cv_x, C_HKV)

    qa = apply_rope(rms_norm(split_heads(aq, A_HQ), qn_a), cos, sin)
    ka = apply_rope(rms_norm(split_heads(ak, A_HKV), kn_a), cos, sin)
    y_a = global_attention(qa, ka, split_heads(av, A_HKV), ka_x, va_x)

    y_f = fourier_mix(fu)

    qc = apply_rope(rms_norm(split_heads(cq, C_HQ), qn_c), cos, sin)
    kc = apply_rope(rms_norm(split_heads(ck, C_HKV), kn_c), cos, sin)
    y_c = window_attention(qc, kc, split_heads(cv, C_HKV), kc_x, vc_x, sink)

    lb_fw = lb[0].reshape(G_HEADS, G_DK)
    lb_bw = lb[1].reshape(G_HEADS, G_DK)
    s0 = jnp.zeros((h.shape[0], G_HEADS, G_DK, G_DV), jnp.float32)
    qg_x, vg_x = jax.nn.silu(gh(gq_x)), gh(gi_x)
    og_fw_x, s_fw = hgrn_scan(qg_x, gh(gfw_x), vg_x, lb_fw, s0, False)
    og_bw_x, s_bw = hgrn_scan(qg_x, gh(gbw_x), vg_x, lb_bw, s0, True)
    qg, vg = jax.nn.silu(gh(gq)), gh(gi)
    og_fw, _ = hgrn_scan(qg, gh(gfw), vg, lb_fw, s_fw, False)
    og_bw, _ = hgrn_scan(qg, gh(gbw), vg, lb_bw, s_bw, True)
    y_g = hgrn_readout(og_fw + og_bw, gg, onorm_g)

    y = merge_branches((y_a, y_f, y_c, y_g), gate, w_branch, w_out)
    if not ctx_out:
        return y, None

    ya_x = context_attention(rms_norm(split_heads(aq_x, A_HQ), qn_a), ka_x, va_x, None)
    yf_x = fourier_mix(fu_x)
    yc_x = context_attention(rms_norm(split_heads(cq_x, C_HQ), qn_c), kc_x, vc_x, sink)
    yg_x = hgrn_readout(og_fw_x + og_bw_x, gg_x, onorm_g)
    y_x = merge_branches((ya_x, yf_x, yc_x, yg_x), gate_x, w_branch, w_out)
    return y, y_x


def moe_ffn(h, router_w, router_b, w1, w3, w2):
    shape = h.shape
    t = h.reshape(-1, shape[-1])
    n_tok = t.shape[0]
    scores = jax.nn.sigmoid((t @ router_w).astype(jnp.float32))
    biased = scores + router_b.astype(jnp.float32)
    grp = biased.reshape(n_tok, N_GROUPS, EXPERTS_PER_GROUP)
    group_score = jnp.sum(lax.top_k(grp, TOP_K)[0], axis=-1)
    g_sel = jnp.argmax(group_score, axis=-1)
    in_group = jnp.take_along_axis(grp, g_sel[:, None, None], axis=1)[:, 0, :]
    _, local_idx = lax.top_k(in_group, TOP_K)
    idx = g_sel[:, None] * EXPERTS_PER_GROUP + local_idx
    w = jnp.take_along_axis(scores, idx, axis=-1)
    w = w / jnp.sum(w, axis=-1, keepdims=True)
    gates = jnp.sum(jax.nn.one_hot(idx, N_EXPERTS, dtype=jnp.float32) * w[..., None], axis=1).astype(t.dtype)
    y = gates[:, 0:1] * ((jax.nn.silu(t @ w1[0]) * (t @ w3[0])) @ w2[0])
    for e in range(1, N_EXPERTS):
        y = y + gates[:, e:e + 1] * ((jax.nn.silu(t @ w1[e]) * (t @ w3[e])) @ w2[e])
    return y.reshape(shape)


def setup_inputs(seed: int = 0) -> dict:
    key = jax.random.key(seed)
    ks = jax.random.split(key, 24)
    f32 = jnp.float32

    def nrm(k, shape, fan_in, gain=1.0):
        return jax.random.normal(k, shape, f32) * (gain * fan_in ** -0.5)

    def gain_vec(k, shape):
        return 1.0 + 0.05 * jax.random.normal(k, shape, f32)

    return {
        'x': jax.random.normal(ks[0], (BATCH, SEQ, D_MODEL), f32),
        'c': jax.random.normal(ks[1], (BATCH, D_MODEL), f32),
        'ctx': jax.random.normal(ks[2], (BATCH, CTX_LEN, D_MODEL), f32),
        'c_ctx': jax.random.normal(ks[3], (D_MODEL,), f32),
        'mod_w': nrm(ks[4], (DEPTH, D_MODEL, MOD_CHUNKS * D_MODEL), D_MODEL, 0.5),
        'mod_b': 0.01 * jax.random.normal(ks[5], (DEPTH, MOD_CHUNKS * D_MODEL), f32),
        'norm1_g': gain_vec(ks[6], (DEPTH, D_MODEL)),
        'norm2_g': gain_vec(ks[7], (DEPTH, D_MODEL)),
        'w_in': nrm(ks[8], (DEPTH, D_MODEL, IN_COLS), D_MODEL),
        'qnorm_a': gain_vec(ks[9], (DEPTH, HEAD_DIM)),
        'knorm_a': gain_vec(ks[10], (DEPTH, HEAD_DIM)),
        'qnorm_c': gain_vec(ks[11], (DEPTH, HEAD_DIM)),
        'knorm_c': gain_vec(ks[12], (DEPTH, HEAD_DIM)),
        'sink_c': 0.5 * jax.random.normal(ks[13], (DEPTH, C_HQ), f32),
        'hgrn_lb_logits': 0.5 * jax.random.normal(ks[14], (2, DEPTH, G_HEADS * G_DK), f32),
        'hgrn_onorm_g': gain_vec(ks[15], (DEPTH, G_DV)),
        'w_branch': nrm(ks[16], (DEPTH, N_BRANCH, BRANCH_W, D_MODEL), BRANCH_W),
        'w_out': nrm(ks[17], (DEPTH, D_MODEL, D_MODEL), D_MODEL),
        'router_w': nrm(ks[18], (D_MODEL, N_EXPERTS), D_MODEL),
        'router_b': 0.01 * jax.random.normal(ks[19], (N_EXPERTS,), f32),
        'moe_w1': nrm(ks[20], (DEPTH, N_EXPERTS, D_MODEL, D_EXPERT), D_MODEL),
        'moe_w3': nrm(ks[21], (DEPTH, N_EXPERTS, D_MODEL, D_EXPERT), D_MODEL),
        'moe_w2': nrm(ks[22], (DEPTH, N_EXPERTS, D_EXPERT, D_MODEL), D_EXPERT),
    }


def reference(x, c, ctx, c_ctx, mod_w, mod_b, norm1_g, norm2_g, w_in, qnorm_a, knorm_a, qnorm_c, knorm_c,
              sink_c, hgrn_lb_logits, hgrn_onorm_g, w_branch, w_out, router_w, router_b, moe_w1, moe_w3, moe_w2):
    rows = x.shape[1] // GRID_W
    cos, sin = axial_rope_tables(rows)
    lower_bounds = hgrn_lower_bounds(hgrn_lb_logits)
    cond = jax.nn.silu(c)[:, None, :]
    cond_ctx = jax.nn.silu(c_ctx)[None, None, :]
    xc = ctx
    for l in range(DEPTH):
        last = l == DEPTH - 1
        m = jnp.split(cond @ mod_w[l] + mod_b[l], MOD_CHUNKS, axis=-1)
        mc = jnp.split(cond_ctx @ mod_w[l] + mod_b[l], MOD_CHUNKS, axis=-1)
        h = modulate(rms_norm(x, norm1_g[l]), m[0], m[1])
        hc = modulate(rms_norm(xc, norm1_g[l]), mc[0], mc[1])
        y, y_x = mixing_sublayer(h, hc, w_in[l], qnorm_a[l], knorm_a[l], qnorm_c[l], knorm_c[l], sink_c[l],
                                 lower_bounds[:, l], hgrn_onorm_g[l], w_branch[l], w_out[l], cos, sin, not last)
        x = x + m[2] * y
        h = modulate(rms_norm(x, norm2_g[l]), m[3], m[4])
        x = x + m[5] * moe_ffn(h, router_w, router_b, moe_w1[l], moe_w3[l], moe_w2[l])
        if not last:
            xc = xc + mc[2] * y_x
            hc = modulate(rms_norm(xc, norm2_g[l]), mc[3], mc[4])
            xc = xc + mc[5] * moe_ffn(hc, router_w, router_b, moe_w1[l], moe_w3[l], moe_w2[l])
    return x
```

```python
import functools
import math

import jax
import jax.numpy as jnp
from jax import lax
from jax.experimental import pallas as pl
from jax.experimental.pallas import tpu as pltpu

F32 = jnp.float32
BF16 = jnp.bfloat16

D_MODEL = 1024
GRID_W = 64
HEAD_DIM = 64
BRANCH_W = 256
N_BRANCH = 4
HQ = 4
HKV = 2
GQA = HQ // HKV
WINDOW = 128
F_GROUP_W = 64
G_HEADS = 4
G_DK = 64
G_CHUNK = 32
N_EXPERTS = 16
N_GROUPS = 4
EXPERTS_PER_GROUP = N_EXPERTS // N_GROUPS
D_EXPERT = 256
ROPE_THETA = 10000.0
EPS = 1e-6
MOD_CHUNKS = 6

C_AQ, C_AK, C_AV, C_FU, C_CQ, C_CK, C_CV = 0, 256, 384, 512, 768, 1024, 1152
C_GQ, C_GATE, IN_COLS = 1280, 2560, 6656

TM = 256
TQ = 128
KV_CHUNK = 256
HG_BLOCK = 256
HG_NCH = HG_BLOCK // G_CHUNK
VMEM_LIMIT = 56 * 1024 * 1024


def _cparams(sem):
    return pltpu.CompilerParams(dimension_semantics=sem, vmem_limit_bytes=VMEM_LIMIT)


def _resident(shape, index_map):
    return pl.BlockSpec(shape, index_map, pipeline_mode=pl.Buffered(1))


def _sigmoid(x):
    return 1.0 / (1.0 + jnp.exp(-x))


def _dot(a, b):
    return jnp.dot(a, b, preferred_element_type=F32)


def _dot_nt(a, b):
    return lax.dot_general(a, b, (((1,), (1,)), ((), ())), preferred_element_type=F32)


def _split3(x):
    hi = x.astype(BF16)
    r1 = x - hi.astype(F32)
    mid = r1.astype(BF16)
    lo = (r1 - mid.astype(F32)).astype(BF16)
    return hi, mid, lo


def _dot01_left(m01, x):
    hi, mid, lo = _split3(x)
    return _dot(m01, hi) + _dot(m01, mid) + _dot(m01, lo)


def _dot01_right(x, m01):
    hi, mid, lo = _split3(x)
    return _dot(hi, m01) + _dot(mid, m01) + _dot(lo, m01)


def _mod_kernel(c_ref, w_ref, b_ref, o_ref):
    c = c_ref[...]
    cond = c * _sigmoid(c)
    o_ref[0] = _dot(cond.astype(BF16), w_ref[0].astype(BF16)) + b_ref[0]


def _mod_vectors(c_all, mod_w, mod_b):
    depth, d, n = mod_w.shape
    rows = c_all.shape[0]
    tn = 1536
    return pl.pallas_call(
        _mod_kernel,
        out_shape=jax.ShapeDtypeStruct((depth, rows, n), F32),
        grid=(depth, n // tn),
        in_specs=[
            pl.BlockSpec((rows, d), lambda l, j: (0, 0)),
            pl.BlockSpec((1, d, tn), lambda l, j: (l, 0, j)),
            pl.BlockSpec((1, 1, tn), lambda l, j: (l, 0, j)),
        ],
        out_specs=pl.BlockSpec((1, rows, tn), lambda l, j: (l, 0, j)),
        compiler_params=_cparams(("arbitrary", "arbitrary")),
        name="mod_vectors",
    )(c_all, mod_w, mod_b.reshape(depth, 1, n))


def _p1_kernel(x_ref, mod_ref, g1_ref, w_ref, cos_ref, s1_ref, s2_ref, qkg_ref, lb_ref,
               qa_ref, ka_ref, va_ref, fu_ref, qc_ref, kc_ref, vc_ref,
               qs_ref, lf_ref, kk_ref, vg_ref, gs_ref, sg_ref):
    x = x_ref[0]
    mod = mod_ref[0]
    shift, scale = mod[0:1, :], mod[1:2, :]
    ms = jnp.mean(x * x, axis=-1, keepdims=True)
    h = (x * lax.rsqrt(ms + EPS) * g1_ref[...]) * (1.0 + scale) + shift
    hb = h.astype(BF16)

    cosv, s1v, s2v = cos_ref[...], s1_ref[...], s2_ref[...]
    first = lax.broadcasted_iota(jnp.int32, (TM, 128), 1) < HEAD_DIM

    def qk_norm_rope(slab, gain, out_scale):
        sq = slab * slab
        ss0 = jnp.sum(jnp.where(first, sq, 0.0), axis=-1, keepdims=True)
        ss1 = jnp.sum(jnp.where(first, 0.0, sq), axis=-1, keepdims=True)
        inv = jnp.where(first, lax.rsqrt(ss0 * (1.0 / HEAD_DIM) + EPS), lax.rsqrt(ss1 * (1.0 / HEAD_DIM) + EPS))
        y = slab * inv * gain
        r = y * cosv + pltpu.roll(y, 96, 1) * s1v + pltpu.roll(y, 32, 1) * s2v
        return (r * out_scale).astype(BF16)

    def attn_inputs(col, g_q, g_k, q_ref, k_ref, v_ref):
        a = _dot(hb, w_ref[:, col:col + 512])
        q_ref[0, :, 0:128] = qk_norm_rope(a[:, 0:128], g_q, HEAD_DIM ** -0.5)
        q_ref[0, :, 128:256] = qk_norm_rope(a[:, 128:256], g_q, HEAD_DIM ** -0.5)
        k_ref[0] = qk_norm_rope(a[:, 256:384], g_k, 1.0)
        v_ref[0] = a[:, 384:512].astype(BF16)

    attn_inputs(C_AQ, qkg_ref[0:1, :], qkg_ref[1:2, :], qa_ref, ka_ref, va_ref)
    fu_ref[0] = _dot(hb, w_ref[:, C_FU:C_FU + 256]).astype(BF16)
    attn_inputs(C_CQ, qkg_ref[2:3, :], qkg_ref[3:4, :], qc_ref, kc_ref, vc_ref)

    g = _dot(hb, w_ref[:, C_GQ:C_GQ + 1280])
    gq = g[:, 0:256]
    qs_ref[0] = (gq * _sigmoid(gq)).astype(BF16)
    for d in range(2):
        lbv = lb_ref[d:d + 1, :]
        f = lbv + (1.0 - lbv) * _sigmoid(g[:, 256 + 256 * d:512 + 256 * d])
        lf_ref[0, :, 256 * d:256 * d + 256] = jnp.log(f)
        kk_ref[0, :, 256 * d:256 * d + 256] = (1.0 - f).astype(BF16)
    vg_ref[0] = g[:, 768:1024].astype(BF16)
    gg = g[:, 1024:1280]
    gs_ref[0] = (gg * _sigmoid(gg)).astype(BF16)

    for j in range(4):
        a = _dot(hb, w_ref[:, C_GATE + 1024 * j:C_GATE + 1024 * (j + 1)])
        sg_ref[0, :, 1024 * j:1024 * (j + 1)] = _sigmoid(a).astype(BF16)


def _input_projection(xs, modsel, g1, w_in, rope, qkg, lb, nt_lat):
    b, p, d = xs.shape
    nt = p // TM
    tok = lambda w: pl.BlockSpec((1, TM, w), lambda i, t: (i, t, 0))
    widths = (256, 128, 128, 256, 256, 128, 128, 256, 512, 512, 256, 256, 4096)
    dtypes = (BF16,) * 8 + (F32,) + (BF16,) * 4
    return pl.pallas_call(
        _p1_kernel,
        out_shape=[jax.ShapeDtypeStruct((b, p, w), dt) for w, dt in zip(widths, dtypes)],
        grid=(b, nt),
        in_specs=[
            tok(d),
            pl.BlockSpec((1, MOD_CHUNKS, d), lambda i, t: (2 * i + t // nt_lat, 0, 0)),
            pl.BlockSpec((1, d), lambda i, t: (0, 0)),
            _resident((d, IN_COLS), lambda i, t: (0, 0)),
            pl.BlockSpec((TM, 128), lambda i, t: (t, 0)),
            pl.BlockSpec((TM, 128), lambda i, t: (t, 0)),
            pl.BlockSpec((TM, 128), lambda i, t: (t, 0)),
            pl.BlockSpec((4, 128), lambda i, t: (0, 0)),
            pl.BlockSpec((2, 256), lambda i, t: (0, 0)),
        ],
        out_specs=[tok(w) for w in widths],
        compiler_params=_cparams(("arbitrary", "arbitrary")),
        name="input_projection",
    )(xs, modsel, g1, w_in, rope[0], rope[1], rope[2], qkg, lb)


def _attn_global_kernel(q_ref, k_ref, v_ref, o_ref, *, seq, ctx):
    qi = pl.program_id(2)
    q = q_ref[0, 0].reshape(GQA * TQ, HEAD_DIM)
    n_lat = seq // KV_CHUNK
    n_all = (seq + ctx) // KV_CHUNK
    lo = jnp.where(qi >= seq // TQ, n_lat, 0)

    def body(c, carry):
        m, l, acc = carry
        start = pl.multiple_of(c * KV_CHUNK, KV_CHUNK)
        kc = k_ref[0, 0, pl.ds(start, KV_CHUNK), :]
        vc = v_ref[0, 0, pl.ds(start, KV_CHUNK), :]
        s = _dot_nt(q, kc)
        m_new = jnp.maximum(m, jnp.max(s, axis=-1, keepdims=True))
        alpha = jnp.exp(m - m_new)
        p = jnp.exp(s - m_new)
        l = alpha * l + jnp.sum(p, axis=-1, keepdims=True)
        acc = alpha * acc + _dot(p.astype(BF16), vc)
        return m_new, l, acc

    init = (jnp.full((GQA * TQ, 1), -jnp.inf, F32), jnp.zeros((GQA * TQ, 1), F32),
            jnp.zeros((GQA * TQ, HEAD_DIM), F32))
    _, l, acc = lax.fori_loop(lo, n_all, body, init)
    o_ref[0, 0] = (acc / l).reshape(GQA, TQ, HEAD_DIM).astype(BF16)


def _attn_window_kernel(q_ref, k_ref, v_ref, sink_ref, o_ref, *, seq, ctx):
    qi = pl.program_id(2)
    q = q_ref[0, 0].reshape(GQA * TQ, HEAD_DIM)
    start = pl.multiple_of(jnp.clip((qi - 1) * TQ, 0, seq - 3 * TQ), TQ)
    k_loc = k_ref[0, 0, pl.ds(start, 3 * TQ), :]
    v_loc = v_ref[0, 0, pl.ds(start, 3 * TQ), :]
    k_ctx = k_ref[0, 0, seq:seq + ctx, :]
    v_ctx = v_ref[0, 0, seq:seq + ctx, :]

    row = lax.broadcasted_iota(jnp.int32, (GQA * TQ, 3 * TQ), 0)
    col = lax.broadcasted_iota(jnp.int32, (GQA * TQ, 3 * TQ), 1)
    q_pos = qi * TQ + (row & (TQ - 1))
    k_pos = start + col
    valid = (jnp.abs(k_pos - q_pos) <= WINDOW) & (qi < seq // TQ)

    s_loc = jnp.where(valid, _dot_nt(q, k_loc), -jnp.inf)
    s_ctx = _dot_nt(q, k_ctx)
    sink = sink_ref[0]
    m = jnp.maximum(jnp.maximum(jnp.max(s_loc, axis=-1, keepdims=True),
                                jnp.max(s_ctx, axis=-1, keepdims=True)), sink)
    p_loc = jnp.exp(s_loc - m)
    p_ctx = jnp.exp(s_ctx - m)
    l = (jnp.sum(p_loc, axis=-1, keepdims=True) + jnp.sum(p_ctx, axis=-1, keepdims=True)
         + jnp.exp(sink - m))
    acc = _dot(p_loc.astype(BF16), v_loc) + _dot(p_ctx.astype(BF16), v_ctx)
    o_ref[0, 0] = (acc / l).reshape(GQA, TQ, HEAD_DIM).astype(BF16)


def _attention(q, k, v, sink, seq, ctx):
    b, _, _, p, _ = q.shape
    q_spec = pl.BlockSpec((1, 1, GQA, TQ, HEAD_DIM), lambda i, h, t: (i, h, 0, t, 0))
    kv_spec = pl.BlockSpec((1, 1, p, HEAD_DIM), lambda i, h, t: (i, h, 0, 0))
    common = dict(
        out_shape=jax.ShapeDtypeStruct(q.shape, BF16),
        grid=(b, HKV, p // TQ),
        out_specs=q_spec,
        compiler_params=_cparams(("arbitrary", "arbitrary", "arbitrary")),
    )
    if sink is None:
        return pl.pallas_call(
            functools.partial(_attn_global_kernel, seq=seq, ctx=ctx),
            in_specs=[q_spec, kv_spec, kv_spec], name="attn_global", **common)(q, k, v)
    return pl.pallas_call(
        functools.partial(_attn_window_kernel, seq=seq, ctx=ctx),
        in_specs=[q_spec, kv_spec, kv_spec, pl.BlockSpec((1, GQA * TQ, 1), lambda i, h, t: (h, 0, 0))],
        name="attn_window", **common)(q, k, v, sink)


def _heads_q(t):
    b, p, _ = t.shape
    return t.reshape(b, p, HKV, GQA, HEAD_DIM).transpose(0, 2, 3, 1, 4)


def _heads_kv(t):
    b, p, _ = t.shape
    return t.reshape(b, p, HKV, HEAD_DIM).transpose(0, 2, 1, 3)


def _unheads(o):
    b, _, _, p, _ = o.shape
    return o.transpose(0, 3, 1, 2, 4).reshape(b, p, HQ * HEAD_DIM)


def _fnet_kernel(u_ref, cs_ref, dft_ref, o_ref, z_ref, *, length, scale):
    @pl.when(pl.program_id(1) == 0)
    def _():
        step = min(length, 512)
        for r in range(0, length, step):
            z = _dot(u_ref[0, r:r + step, :], cs_ref[...])
            z_ref[r:r + step, :] = z[:, 0:256].astype(BF16)
            z_ref[length + r:length + r + step, :] = z[:, 256:512].astype(BF16)

    o_ref[0] = (_dot(dft_ref[...], z_ref[...]) * scale).astype(BF16)


def _dft_tables(length):
    def cs(n, m):
        idx = jnp.arange(n, dtype=jnp.int32)
        ang = ((idx[:, None] * idx[None, :]) % m).astype(F32) * (2.0 * math.pi / m)
        return jnp.cos(ang), jnp.sin(ang)

    cc, sc = cs(F_GROUP_W, F_GROUP_W)
    eye = jnp.eye(BRANCH_W // F_GROUP_W, dtype=F32)
    chan = jnp.concatenate([jnp.kron(eye, cc), jnp.kron(eye, sc)], axis=1).astype(BF16)
    cl, sl = cs(length, length)
    pos = jnp.concatenate([cl, -sl], axis=1).astype(BF16)
    return chan, pos


def _fourier_mix(u, row_block, length, tables):
    b = u.shape[0]
    chan, pos = tables
    tm = min(length, 512)
    return pl.pallas_call(
        functools.partial(_fnet_kernel, length=length, scale=(length * F_GROUP_W) ** -0.5),
        out_shape=jax.ShapeDtypeStruct((b, length, BRANCH_W), BF16),
        grid=(b, length // tm),
        in_specs=[
            pl.BlockSpec((1, length, BRANCH_W), lambda i, t: (i, row_block, 0)),
            pl.BlockSpec((BRANCH_W, 2 * BRANCH_W), lambda i, t: (0, 0)),
            pl.BlockSpec((tm, 2 * length), lambda i, t: (t, 0)),
        ],
        out_specs=pl.BlockSpec((1, tm, BRANCH_W), lambda i, t: (i, t, 0)),
        scratch_shapes=[pltpu.VMEM((2 * length, BRANCH_W), BF16)],
        compiler_params=_cparams(("arbitrary", "arbitrary")),
        name="fourier_mix",
    )(u, chan, pos)


def _hgrn_kernel(q_ref, k_ref, lf_ref, v_ref, tri_ref, blk_ref, o_ref, s_ref, kv_ref, sp_ref, *, reverse):
    @pl.when(pl.program_id(1) == 0)
    def _():
        s_ref[...] = jnp.zeros_like(s_ref)

    r, w = HG_BLOCK, G_HEADS * G_DK
    lf = lf_ref[0]
    tri = tri_ref[...]
    cum = _dot01_left(tri, lf)
    tot = _dot01_left(blk_ref[...], lf)
    qf = q_ref[0].astype(F32)
    kf = k_ref[0].astype(F32)
    vf = v_ref[0].astype(F32)
    q_in = qf * jnp.exp(cum)
    k_in_b = (kf * jnp.exp(-cum)).astype(BF16)
    k_out_b = (kf * jnp.exp(tot - cum)).astype(BF16)
    q_in_b = q_in.astype(BF16)

    lane_head = lax.broadcasted_iota(jnp.int32, (r, w), 1) // G_DK
    in_tri = tri > 0

    o = jnp.zeros((r, w), F32)
    for h in range(G_HEADS):
        lm = lane_head == h
        qh = jnp.where(lm, q_in, 0.0).astype(BF16)
        att = jnp.where(in_tri, _dot_nt(qh, k_in_b), 0.0).astype(BF16)
        o = o + _dot(att, jnp.where(lm, vf, 0.0).astype(BF16))

    v_t = vf.T
    col_chunk = lax.broadcasted_iota(jnp.int32, (w, r), 1) // G_CHUNK
    lane_head64 = lax.broadcasted_iota(jnp.int32, (G_DK, w), 1) // G_DK
    for j in range(HG_NCH):
        full = _dot(jnp.where(col_chunk == j, v_t, 0.0).astype(BF16), k_out_b)
        kvp = jnp.zeros((G_DK, w), F32)
        for h in range(G_HEADS):
            kvp = kvp + jnp.where(lane_head64 == h, full[G_DK * h:G_DK * (h + 1), :], 0.0)
        kv_ref[j] = kvp

    s = s_ref[...]
    for j in (range(HG_NCH - 1, -1, -1) if reverse else range(HG_NCH)):
        sp_ref[j] = s
        s = s * jnp.exp(tot[G_CHUNK * j:G_CHUNK * j + 1, :]) + kv_ref[j]
    s_ref[...] = s

    for j in range(HG_NCH):
        sj = sp_ref[j]
        s_bd = jnp.concatenate([jnp.where(lane_head64 == h, sj, 0.0) for h in range(G_HEADS)], axis=0)
        rows = slice(G_CHUNK * j, G_CHUNK * (j + 1))
        o_ref[0, rows, :] = o[rows, :] + _dot_nt(q_in_b[rows, :], s_bd.astype(BF16))


def _hgrn_scan(qs, kk, lf, vg, direction, nt_lat):
    b, p, w = qs.shape
    nb = p // HG_BLOCK
    reverse = direction == 1
    idx = jnp.arange(HG_BLOCK)
    same = (idx[:, None] // G_CHUNK) == (idx[None, :] // G_CHUNK)
    order = (idx[None, :] >= idx[:, None]) if reverse else (idx[None, :] <= idx[:, None])
    tri = (same & order).astype(BF16)
    blk = same.astype(BF16)
    if reverse:
        bmap = lambda i, t: (i, nb - 1 - t, 0)
        dmap = lambda i, t: (i, nb - 1 - t, 1)
    else:
        bmap = lambda i, t: (i, (t + nt_lat) % nb, 0)
        dmap = lambda i, t: (i, (t + nt_lat) % nb, 0)
    blk_spec = pl.BlockSpec((1, HG_BLOCK, w), bmap)
    dir_spec = pl.BlockSpec((1, HG_BLOCK, w), dmap)
    const = pl.BlockSpec((HG_BLOCK, HG_BLOCK), lambda i, t: (0, 0))
    return pl.pallas_call(
        functools.partial(_hgrn_kernel, reverse=reverse),
        out_shape=jax.ShapeDtypeStruct((b, p, w), F32),
        grid=(b, nb),
        in_specs=[blk_spec, dir_spec, dir_spec, blk_spec, const, const],
        out_specs=blk_spec,
        scratch_shapes=[pltpu.VMEM((G_DK, w), F32), pltpu.VMEM((HG_NCH, G_DK, w), F32),
                        pltpu.VMEM((HG_NCH, G_DK, w), F32)],
        compiler_params=_cparams(("arbitrary", "arbitrary")),
        name="hgrn_scan_bw" if reverse else "hgrn_scan_fw",
    )(qs, kk, lf, vg, tri, blk)


def _merge_kernel(ya_ref, yf_ref, yc_ref, ofw_ref, obw_ref, gs_ref, og_ref, sg_ref, wb_ref, wo_ref,
                  x_ref, mod_ref, g2_ref, rw_ref, rb_ref, seg_ref,
                  x1_ref, h2_ref, gt_ref):
    og = ofw_ref[0] + obw_ref[0]
    ssq = _dot01_right(og * og, seg_ref[...])
    yg = og * lax.rsqrt(ssq * (1.0 / G_DK) + EPS) * og_ref[...] * gs_ref[0].astype(F32)

    branches = (ya_ref[0], yf_ref[0], yc_ref[0], yg.astype(BF16))
    merged = None
    for i, y in enumerate(branches):
        term = sg_ref[0, :, D_MODEL * i:D_MODEL * (i + 1)].astype(F32) * _dot(y, wb_ref[i])
        merged = term if merged is None else merged + term
    y = _dot(merged.astype(BF16), wo_ref[...])

    mod = mod_ref[0]
    x1 = x_ref[0] + mod[2:3, :] * y
    x1_ref[0] = x1
    ms = jnp.mean(x1 * x1, axis=-1, keepdims=True)
    h2 = (x1 * lax.rsqrt(ms + EPS) * g2_ref[...]) * (1.0 + mod[4:5, :]) + mod[3:4, :]
    h2b = h2.astype(BF16)
    h2_ref[0] = h2b

    sc = _sigmoid(_dot_nt(rw_ref[...], h2b))
    bi = sc + rb_ref[...]
    s_rows = [sc[e:e + 1, :] for e in range(N_EXPERTS)]
    b_rows = [bi[e:e + 1, :] for e in range(N_EXPERTS)]

    def beats(vj, j, vi, i):
        return (vj > vi) | ((vj == vi) & (j < i)) if j < i else (vj > vi)

    top2, g_score = [], []
    for g in range(N_GROUPS):
        ids = range(g * EXPERTS_PER_GROUP, (g + 1) * EXPERTS_PER_GROUP)
        score = jnp.zeros_like(b_rows[0])
        for i in ids:
            rank = sum(beats(b_rows[j], j, b_rows[i], i).astype(F32) for j in ids if j != i)
            sel = rank < 1.5
            top2.append(sel)
            score = score + jnp.where(sel, b_rows[i], 0.0)
        g_score.append(score)
    chosen = []
    for g in range(N_GROUPS):
        g_rank = sum(beats(g_score[j], j, g_score[g], g).astype(F32) for j in range(N_GROUPS) if j != g)
        g_sel = g_rank < 0.5
        for i in range(g * EXPERTS_PER_GROUP, (g + 1) * EXPERTS_PER_GROUP):
            chosen.append(jnp.where(g_sel & top2[i], s_rows[i], 0.0))
    denom = chosen[0]
    for e in range(1, N_EXPERTS):
        denom = denom + chosen[e]
    for e in range(N_EXPERTS):
        gt_ref[e:e + 1, :] = chosen[e] / denom


def _merge(ya, yf, yc, ofw, obw, gs, og_gain, sg, wb, wo, xs, modsel, g2, rw_t, rb, nt_lat, nt_run):
    b, p, d = xs.shape
    tok = lambda w: pl.BlockSpec((1, TM, w), lambda i, t: (i, t, 0))
    idx = jnp.arange(BRANCH_W)
    seg = ((idx[:, None] // G_DK) == (idx[None, :] // G_DK)).astype(BF16)
    nt = p // TM
    return pl.pallas_call(
        _merge_kernel,
        out_shape=[jax.ShapeDtypeStruct((b, p, d), F32), jax.ShapeDtypeStruct((b, p, d), BF16),
                   jax.ShapeDtypeStruct((N_EXPERTS, b * p), F32)],
        grid=(b, nt_run),
        in_specs=[
            tok(256), tok(256), tok(256), tok(256), tok(256), tok(256),
            pl.BlockSpec((1, BRANCH_W), lambda i, t: (0, 0)),
            tok(N_BRANCH * d),
            _resident((N_BRANCH, BRANCH_W, d), lambda i, t: (0, 0, 0)),
            _resident((d, d), lambda i, t: (0, 0)),
            tok(d),
            pl.BlockSpec((1, MOD_CHUNKS, d), lambda i, t: (2 * i + t // nt_lat, 0, 0)),
            pl.BlockSpec((1, d), lambda i, t: (0, 0)),
            pl.BlockSpec((N_EXPERTS, d), lambda i, t: (0, 0)),
            pl.BlockSpec((N_EXPERTS, 1), lambda i, t: (0, 0)),
            pl.BlockSpec((BRANCH_W, BRANCH_W), lambda i, t: (0, 0)),
        ],
        out_specs=[tok(d), tok(d), pl.BlockSpec((N_EXPERTS, TM), lambda i, t: (0, i * nt + t))],
        compiler_params=_cparams(("arbitrary", "arbitrary")),
        name="merge_router",
    )(ya, yf, yc, ofw, obw, gs, og_gain, sg, wb, wo, xs, modsel, g2, rw_t, rb, seg)


def _moe_kernel(h_ref, gt_ref, w1_ref, w3_ref, w2_ref, x_ref, mod_ref, o_ref):
    hb = h_ref[0]
    gates = gt_ref[0]
    acc = jnp.zeros((TM, D_MODEL), F32)
    for e in range(N_EXPERTS):
        a = _dot(hb, w1_ref[e])
        hid = (a * _sigmoid(a)) * _dot(hb, w3_ref[e])
        acc = acc + _dot((hid * gates[:, e:e + 1]).astype(BF16), w2_ref[e])
    o_ref[0] = x_ref[0] + mod_ref[0][5:6, :] * acc


def _moe(h2, gates, w1, w3, w2, x1, modsel, nt_lat, nt_run, out_rows):
    b, p, d = x1.shape
    tok = lambda w: pl.BlockSpec((1, TM, w), lambda i, t: (i, t, 0))
    return pl.pallas_call(
        _moe_kernel,
        out_shape=jax.ShapeDtypeStruct((b, out_rows, d), F32),
        grid=(b, nt_run),
        in_specs=[
            tok(d), tok(N_EXPERTS),
            _resident(w1.shape, lambda i, t: (0, 0, 0)),
            _resident(w3.shape, lambda i, t: (0, 0, 0)),
            _resident(w2.shape, lambda i, t: (0, 0, 0)),
            tok(d),
            pl.BlockSpec((1, MOD_CHUNKS, d), lambda i, t: (2 * i + t // nt_lat, 0, 0)),
        ],
        out_specs=tok(d),
        compiler_params=_cparams(("arbitrary", "arbitrary")),
        name="moe_ffn",
    )(h2, gates, w1, w3, w2, x1, modsel)


def _rope_tables(seq, ctx):
    t = jnp.arange(seq)
    row = (t // GRID_W).astype(F32)
    col = (t % GRID_W).astype(F32)
    n_freq = HEAD_DIM // 4
    inv_freq = ROPE_THETA ** (-jnp.arange(n_freq, dtype=F32) / n_freq)
    ang = jnp.concatenate([row[:, None] * inv_freq, col[:, None] * inv_freq], axis=-1)
    cos = jnp.concatenate([jnp.cos(ang), jnp.ones((ctx, HEAD_DIM // 2), F32)], axis=0)
    sin = jnp.concatenate([jnp.sin(ang), jnp.zeros((ctx, HEAD_DIM // 2), F32)], axis=0)
    zero = jnp.zeros_like(sin)
    cos_t = jnp.tile(cos, (1, 4))
    s1_t = jnp.tile(jnp.concatenate([-sin, zero], axis=1), (1, 2))
    s2_t = jnp.tile(jnp.concatenate([zero, sin], axis=1), (1, 2))
    return cos_t, s1_t, s2_t


def kernel(x, c, ctx, c_ctx, mod_w, mod_b, norm1_g, norm2_g, w_in, qnorm_a, knorm_a, qnorm_c, knorm_c,
           sink_c, hgrn_lb_logits, hgrn_onorm_g, w_branch, w_out, router_w, router_b, moe_w1, moe_w3, moe_w2):
    b, seq, d = x.shape
    t_ctx = ctx.shape[1]
    depth = mod_w.shape[0]
    p = seq + t_ctx
    assert d == D_MODEL and seq % TM == 0 and t_ctx % TM == 0 and seq >= 3 * TQ and seq % GRID_W == 0
    nt_lat, nt = seq // TM, p // TM

    rows = -(-(b + 1) // 8) * 8
    c_all = jnp.zeros((rows, d), F32).at[:b].set(c).at[b].set(c_ctx)
    mods = _mod_vectors(c_all, mod_w, mod_b).reshape(depth, rows, MOD_CHUNKS, d)
    modsel = jnp.stack([mods[:, :b], jnp.broadcast_to(mods[:, b:b + 1], (depth, b, MOD_CHUNKS, d))], axis=2)
    modsel = modsel.reshape(depth, 2 * b, MOD_CHUNKS, d)

    rope = _rope_tables(seq, t_ctx)
    lb_p = jax.nn.softmax(hgrn_lb_logits.astype(F32), axis=1)
    lower_bounds = jnp.cumsum(lb_p, axis=1) - lb_p[:, :1]
    dft_lat = _dft_tables(seq)
    dft_ctx = _dft_tables(t_ctx)
    rw_t = router_w.T.astype(BF16)
    rb = router_b.astype(F32).reshape(N_EXPERTS, 1)

    xs = jnp.concatenate([x, ctx], axis=1)
    for l in range(depth):
        last = l == depth - 1
        nt_run = nt_lat if last else nt
        qkg = jnp.stack([jnp.tile(v[l], 2) for v in (qnorm_a, knorm_a, qnorm_c, knorm_c)])
        (qa, ka, va, fu, qc, kc, vc, qs, lf, kk, vg, gs, sg) = _input_projection(
            xs, modsel[l], norm1_g[l].reshape(1, d), w_in[l].astype(BF16), rope, qkg, lower_bounds[:, l], nt_lat)

        ya = _unheads(_attention(_heads_q(qa), _heads_kv(ka), _heads_kv(va), None, seq, t_ctx))
        sink = jnp.repeat(sink_c[l].astype(F32).reshape(HKV, GQA), TQ, axis=1).reshape(HKV, GQA * TQ, 1)
        yc = _unheads(_attention(_heads_q(qc), _heads_kv(kc), _heads_kv(vc), sink, seq, t_ctx))
        yf = jnp.concatenate([_fourier_mix(fu, 0, seq, dft_lat),
                              _fourier_mix(fu, seq // t_ctx, t_ctx, dft_ctx)], axis=1)
        ofw = _hgrn_scan(qs, kk, lf, vg, 0, nt_lat)
        obw = _hgrn_scan(qs, kk, lf, vg, 1, nt_lat)

        x1, h2, gates_t = _merge(ya, yf, yc, ofw, obw, gs, jnp.tile(hgrn_onorm_g[l], G_HEADS).reshape(1, BRANCH_W),
                                 sg, w_branch[l].astype(BF16), w_out[l].astype(BF16), xs, modsel[l],
                                 norm2_g[l].reshape(1, d), rw_t, rb, nt_lat, nt_run)
        gates = gates_t.T.reshape(b, p, N_EXPERTS)
        xs = _moe(h2, gates, moe_w1[l].astype(BF16), moe_w3[l].astype(BF16), moe_w2[l].astype(BF16),
                  x1, modsel[l], nt_lat, nt_run, seq if last else p)
    return xs
```
